```python
import math
import jax, jax.numpy as jnp
from jax import lax
import numpy as np

D_MODEL = 2048
BATCH = 2
SEQ = 4096
DEPTH = 1

MIX_WIDTH = D_MODEL
CONV_CH = MIX_WIDTH // 2
CONV_GROUPS = 8
CONV_WIDTH = 31
SG_CH = MIX_WIDTH - CONV_CH
SG_HEADS = 8
SG_HEAD_DIM = SG_CH // SG_HEADS
CHUNK = 128
D_FF = 5632
LN_EPS = 1e-5
DN_ALPHA = (2.0 * DEPTH) ** 0.25
DN_BETA = (8.0 * DEPTH) ** -0.25

kernel_name = "hybrid_conv_sgmlp_macaron_deepnorm"


def layer_norm(x, g, b):
    xf = x.astype(jnp.float32)
    mu = jnp.mean(xf, axis=-1, keepdims=True)
    var = jnp.mean(jnp.square(xf - mu), axis=-1, keepdims=True)
    y = (xf - mu) * lax.rsqrt(var + LN_EPS)
    return (y * g + b).astype(x.dtype)


def swiglu_ffn(x, w_gate_up, w_down):
    gu = x @ w_gate_up
    g, u = jnp.split(gu, 2, axis=-1)
    return (jax.nn.silu(g) * u) @ w_down


def conv_mixer(a_val, a_gate, conv_w, conv_b, ln_g, ln_b):
    h = a_val * jax.nn.sigmoid(a_gate)
    rhs = conv_w[:, None, :].astype(h.dtype)
    h = lax.conv_general_dilated(
        h, rhs, window_strides=(1,), padding=[(CONV_WIDTH - 1, 0)],
        dimension_numbers=("NWC", "WIO", "NWC"),
        feature_group_count=CONV_CH) + conv_b
    h = layer_norm(h, ln_g, ln_b)
    return jax.nn.silu(h)


def spatial_gating_mixer(b_u, b_v, ln_g, ln_b, w_s, b_s):
    bsz, seq, _ = b_u.shape
    n_chunks = seq // CHUNK
    u = jax.nn.gelu(b_u).reshape(bsz, n_chunks, CHUNK, SG_HEADS, SG_HEAD_DIM)
    v = jax.nn.gelu(b_v).reshape(bsz, n_chunks, CHUNK, SG_HEADS, SG_HEAD_DIM)
    v = layer_norm(v, ln_g, ln_b)
    causal = jnp.tril(jnp.ones((CHUNK, CHUNK), dtype=w_s.dtype))
    w = w_s * causal
    mixed = jnp.einsum("hts,bcshd->bcthd", w, v) + b_s.T[None, None, :, :, None]
    return (u * mixed).reshape(bsz, seq, SG_CH)


def setup_inputs(seed: int = 0) -> dict:
    key = jax.random.key(seed)
    ks = jax.random.split(key, 24)
    L = DEPTH
    nrm = lambda k, shape, s: jax.random.normal(k, shape, jnp.float32) * s
    gain = lambda k, shape: 1.0 + nrm(k, shape, 0.02)
    return {
        "x": nrm(ks[0], (BATCH, SEQ, D_MODEL), 1.0),
        "ffn1_w_gate_up": nrm(ks[1], (L, D_MODEL, 2 * D_FF), D_MODEL ** -0.5),
        "ffn1_w_down": nrm(ks[2], (L, D_FF, D_MODEL), DN_BETA * D_FF ** -0.5),
        "ln1_g": gain(ks[3], (L, D_MODEL)),
        "ln1_b": nrm(ks[4], (L, D_MODEL), 0.02),
        "mix_w_in": nrm(ks[5], (L, D_MODEL, 2 * CONV_CH + 2 * SG_CH), D_MODEL ** -0.5),
        "conv_w": nrm(ks[6], (L, CONV_WIDTH, CONV_CH), CONV_WIDTH ** -0.5),
        "conv_b": nrm(ks[7], (L, CONV_CH), 0.02),
        "conv_ln_g": gain(ks[8], (L, CONV_CH)),
        "conv_ln_b": nrm(ks[9], (L, CONV_CH), 0.02),
        "sg_ln_g": gain(ks[10], (L, SG_HEADS, SG_HEAD_DIM)),
        "sg_ln_b": nrm(ks[11], (L, SG_HEADS, SG_HEAD_DIM), 0.02),
        "sg_w": nrm(ks[12], (L, SG_HEADS, CHUNK, CHUNK), CHUNK ** -0.5),
        "sg_b": gain(ks[13], (L, SG_HEADS, CHUNK)),
        "mix_w_out": nrm(ks[14], (L, MIX_WIDTH, D_MODEL), DN_BETA * MIX_WIDTH ** -0.5),
        "ln2_g": gain(ks[15], (L, D_MODEL)),
        "ln2_b": nrm(ks[16], (L, D_MODEL), 0.02),
        "ffn2_w_gate_up": nrm(ks[17], (L, D_MODEL, 2 * D_FF), D_MODEL ** -0.5),
        "ffn2_w_down": nrm(ks[18], (L, D_FF, D_MODEL), DN_BETA * D_FF ** -0.5),
        "ln3_g": gain(ks[19], (L, D_MODEL)),
        "ln3_b": nrm(ks[20], (L, D_MODEL), 0.02),
    }


def reference(x, ffn1_w_gate_up, ffn1_w_down, ln1_g, ln1_b, mix_w_in, conv_w,
              conv_b, conv_ln_g, conv_ln_b, sg_ln_g, sg_ln_b, sg_w, sg_b,
              mix_w_out, ln2_g, ln2_b, ffn2_w_gate_up, ffn2_w_down, ln3_g, ln3_b):
    for l in range(DEPTH):
        x = layer_norm(DN_ALPHA * x + 0.5 * swiglu_ffn(x, ffn1_w_gate_up[l], ffn1_w_down[l]),
                       ln1_g[l], ln1_b[l])
        proj = x @ mix_w_in[l]
        a_val, a_gate, b_u, b_v = jnp.split(
            proj, [CONV_CH, 2 * CONV_CH, 2 * CONV_CH + SG_CH], axis=-1)
        y_a = conv_mixer(a_val, a_gate, conv_w[l], conv_b[l], conv_ln_g[l], conv_ln_b[l])
        y_b = spatial_gating_mixer(b_u, b_v, sg_ln_g[l], sg_ln_b[l], sg_w[l], sg_b[l])
        mix = jnp.concatenate([y_a, y_b], axis=-1) @ mix_w_out[l]
        x = layer_norm(DN_ALPHA * x + mix, ln2_g[l], ln2_b[l])
        x = layer_norm(DN_ALPHA * x + 0.5 * swiglu_ffn(x, ffn2_w_gate_up[l], ffn2_w_down[l]),
                       ln3_g[l], ln3_b[l])
    return x
```

```python
import functools
import math

import jax
import jax.numpy as jnp
from jax import lax
from jax.experimental import pallas as pl
from jax.experimental.pallas import tpu as pltpu

D_MODEL = 2048
SEQ = 4096
CONV_CH = 1024
CONV_WIDTH = 31
SG_CH = 1024
SG_HEADS = 8
SG_HEAD_DIM = 128
CHUNK = 128
D_FF = 5632
LN_EPS = 1e-5
DN_ALPHA = 2.0 ** 0.25
GELU_C = math.sqrt(2.0 / math.pi)

LANES = 128
HALO = 32
VMEM_LIMIT_BYTES = 58 * 1024 * 1024

FFN_TM = 1024
FFN_TF = 512
MIX_TM = 512
LN_ROWS = 64
CONV_ROWS = 32
ACT_ROWS = 32

F32 = jnp.float32
BF16 = jnp.bfloat16


def _sigmoid(x):
    return 1.0 / (1.0 + jnp.exp(-x))


def _gelu_tanh(x):
    return x * (0.5 * (1.0 + jnp.tanh(GELU_C * (x + 0.044715 * (x * x * x)))))


def _layer_norm(y, g, b):
    mu = jnp.mean(y, axis=-1, keepdims=True)
    yc = y - mu
    var = jnp.mean(yc * yc, axis=-1, keepdims=True)
    return yc * lax.rsqrt(var + LN_EPS) * g + b


def _residual_ln_rows(x_ref, acc_ref, g_ref, b_ref, o_ref, scale, n_rows):
    def body(r, carry):
        rows = pl.ds(pl.multiple_of(r * LN_ROWS, LN_ROWS), LN_ROWS)
        y = DN_ALPHA * x_ref[rows, :] + scale * acc_ref[rows, :]
        o_ref[rows, :] = _layer_norm(y, g_ref[...], b_ref[...])
        return carry
    lax.fori_loop(0, n_rows // LN_ROWS, body, 0)


def _ffn_ln_kernel(x_ref, wg_ref, wu_ref, wd_ref, g_ref, b_ref, o_ref, xb_ref):
    j = pl.program_id(1)

    @pl.when(j == 0)
    def _():
        xb_ref[...] = x_ref[...].astype(BF16)
        o_ref[...] = jnp.zeros_like(o_ref)

    xb = xb_ref[...]
    gate = jnp.dot(xb, wg_ref[...], preferred_element_type=F32)
    up = jnp.dot(xb, wu_ref[...], preferred_element_type=F32)
    h = (gate * _sigmoid(gate) * up).astype(BF16)
    o_ref[...] += jnp.dot(h, wd_ref[...], preferred_element_type=F32)

    @pl.when(j == pl.num_programs(1) - 1)
    def _():
        _residual_ln_rows(x_ref, o_ref, g_ref, b_ref, o_ref, 0.5, FFN_TM)


def _ffn_ln(x, w_gate_up, w_down, ln_g, ln_b):
    m = x.shape[0]
    nf = D_FF // FFN_TF
    return pl.pallas_call(
        _ffn_ln_kernel,
        out_shape=jax.ShapeDtypeStruct((m, D_MODEL), F32),
        grid=(m // FFN_TM, nf),
        in_specs=[
            pl.BlockSpec((FFN_TM, D_MODEL), lambda i, j: (i, 0)),
            pl.BlockSpec((D_MODEL, FFN_TF), lambda i, j: (0, j)),
            pl.BlockSpec((D_MODEL, FFN_TF), lambda i, j: (0, j + nf)),
            pl.BlockSpec((FFN_TF, D_MODEL), lambda i, j: (j, 0)),
            pl.BlockSpec((1, D_MODEL), lambda i, j: (0, 0)),
            pl.BlockSpec((1, D_MODEL), lambda i, j: (0, 0)),
        ],
        out_specs=pl.BlockSpec((FFN_TM, D_MODEL), lambda i, j: (i, 0)),
        scratch_shapes=[pltpu.VMEM((FFN_TM, D_MODEL), BF16)],
        compiler_params=pltpu.CompilerParams(
            dimension_semantics=("arbitrary", "arbitrary"),
            vmem_limit_bytes=VMEM_LIMIT_BYTES,
        ),
        name="ffn_ln",
    )(x, w_gate_up, w_gate_up, w_down, ln_g, ln_b)


def _mixer_ln_kernel(x_ref, win_ref, cw_ref, cb_ref, cg_ref, cbeta_ref,
                     sg_g_ref, sg_beta_ref, sgw_ref, sgbt_ref, wout_ref,
                     g_ref, b_ref, o_ref,
                     xb_ref, p_ref, hist_ref, y_ref, vb_ref, wm_ref, bias_ref):
    i = pl.program_id(0)
    tiles_per_seq = SEQ // MIX_TM
    n_ct = CONV_CH // LANES

    @pl.when(i == 0)
    def _():
        t_idx = lax.broadcasted_iota(jnp.int32, (CHUNK, CHUNK), 0)
        s_idx = lax.broadcasted_iota(jnp.int32, (CHUNK, CHUNK), 1)
        for h in range(SG_HEADS):
            wm_ref[h] = jnp.where(s_idx <= t_idx, sgw_ref[h], 0.0).astype(BF16)
            bias_ref[:, h * LANES:(h + 1) * LANES] = jnp.broadcast_to(
                sgbt_ref[:, h:h + 1], (CHUNK, LANES))

    @pl.when(i % tiles_per_seq == 0)
    def _():
        hist_ref[:, 0:HALO, :] = jnp.zeros((n_ct, HALO, LANES), F32)

    xb_ref[...] = x_ref[...].astype(BF16)

    p_ref[...] = jnp.dot(xb_ref[...], win_ref[:, 0:2 * CONV_CH], preferred_element_type=F32)

    def glu_body(r, carry):
        r0 = pl.multiple_of(r * ACT_ROWS, ACT_ROWS)
        rows = pl.ds(r0, ACT_ROWS)
        hval = p_ref[rows, 0:CONV_CH] * _sigmoid(p_ref[rows, CONV_CH:2 * CONV_CH])
        for c in range(n_ct):
            hist_ref[c, pl.ds(HALO + r0, ACT_ROWS), :] = hval[:, c * LANES:(c + 1) * LANES]
        return carry
    lax.fori_loop(0, MIX_TM // ACT_ROWS, glu_body, 0)

    tap0 = HALO - (CONV_WIDTH - 1)
    for c in range(n_ct):
        lanes = slice(c * LANES, (c + 1) * LANES)
        taps = [cw_ref[k:k + 1, lanes] for k in range(CONV_WIDTH)]
        cbias = cb_ref[:, lanes]

        def conv_body(r, carry, c=c, lanes=lanes, taps=taps, cbias=cbias):
            r0 = pl.multiple_of(r * CONV_ROWS, CONV_ROWS)
            acc = taps[0] * hist_ref[c, pl.ds(r0 + tap0, CONV_ROWS), :]
            for k in range(1, CONV_WIDTH):
                acc = acc + taps[k] * hist_ref[c, pl.ds(r0 + tap0 + k, CONV_ROWS), :]
            p_ref[pl.ds(r0, CONV_ROWS), lanes] = acc + cbias
            return carry
        lax.fori_loop(0, MIX_TM // CONV_ROWS, conv_body, 0)

    for c in range(n_ct):
        hist_ref[c, 0:HALO, :] = hist_ref[c, MIX_TM:MIX_TM + HALO, :]

    def conv_ln_body(r, carry):
        rows = pl.ds(pl.multiple_of(r * LN_ROWS, LN_ROWS), LN_ROWS)
        z = _layer_norm(p_ref[rows, 0:CONV_CH], cg_ref[...], cbeta_ref[...])
        y_ref[rows, 0:CONV_CH] = (z * _sigmoid(z)).astype(BF16)
        return carry
    lax.fori_loop(0, MIX_TM // LN_ROWS, conv_ln_body, 0)

    p_ref[...] = jnp.dot(xb_ref[...], win_ref[:, 2 * CONV_CH:2 * CONV_CH + 2 * SG_CH],
                         preferred_element_type=F32)

    def sg_act_body(r, carry):
        rows = pl.ds(pl.multiple_of(r * ACT_ROWS, ACT_ROWS), ACT_ROWS)
        p_ref[rows, 0:SG_CH] = _gelu_tanh(p_ref[rows, 0:SG_CH])
        v = _gelu_tanh(p_ref[rows, SG_CH:2 * SG_CH])
        for h in range(SG_HEADS):
            lanes = slice(h * LANES, (h + 1) * LANES)
            vb_ref[rows, lanes] = _layer_norm(
                v[:, lanes], sg_g_ref[:, lanes], sg_beta_ref[:, lanes]).astype(BF16)
        return carry
    lax.fori_loop(0, MIX_TM // ACT_ROWS, sg_act_body, 0)

    for ck in range(MIX_TM // CHUNK):
        rows = slice(ck * CHUNK, (ck + 1) * CHUNK)
        for h in range(SG_HEADS):
            lanes = slice(h * LANES, (h + 1) * LANES)
            mixed = jnp.dot(wm_ref[h], vb_ref[rows, lanes], preferred_element_type=F32)
            mixed = mixed + bias_ref[:, lanes]
            y_ref[rows, CONV_CH + h * LANES:CONV_CH + (h + 1) * LANES] = (
                p_ref[rows, lanes] * mixed).astype(BF16)

    p_ref[...] = jnp.dot(y_ref[...], wout_ref[...], preferred_element_type=F32)
    _residual_ln_rows(x_ref, p_ref, g_ref, b_ref, o_ref, 1.0, MIX_TM)


def _mixer_ln(x, w_in, conv_w, conv_b, conv_ln_g, conv_ln_b, sg_ln_g, sg_ln_b,
              sg_w, sg_b_t, w_out, ln_g, ln_b):
    m = x.shape[0]
    const2 = lambda i: (0, 0)
    resident = functools.partial(pl.BlockSpec, pipeline_mode=pl.Buffered(1))
    return pl.pallas_call(
        _mixer_ln_kernel,
        out_shape=jax.ShapeDtypeStruct((m, D_MODEL), F32),
        grid=(m // MIX_TM,),
        in_specs=[
            pl.BlockSpec((MIX_TM, D_MODEL), lambda i: (i, 0)),
            resident((D_MODEL, 2 * CONV_CH + 2 * SG_CH), const2),
            pl.BlockSpec((CONV_WIDTH, CONV_CH), const2),
            pl.BlockSpec((1, CONV_CH), const2),
            pl.BlockSpec((1, CONV_CH), const2),
            pl.BlockSpec((1, CONV_CH), const2),
            pl.BlockSpec((1, SG_CH), const2),
            pl.BlockSpec((1, SG_CH), const2),
            pl.BlockSpec((SG_HEADS, CHUNK, CHUNK), lambda i: (0, 0, 0)),
            pl.BlockSpec((CHUNK, SG_HEADS), const2),
            resident((2 * CONV_CH, D_MODEL), const2),
            pl.BlockSpec((1, D_MODEL), const2),
            pl.BlockSpec((1, D_MODEL), const2),
        ],
        out_specs=pl.BlockSpec((MIX_TM, D_MODEL), lambda i: (i, 0)),
        scratch_shapes=[
            pltpu.VMEM((MIX_TM, D_MODEL), BF16),
            pltpu.VMEM((MIX_TM, 2 * CONV_CH), F32),
            pltpu.VMEM((CONV_CH // LANES, HALO + MIX_TM, LANES), F32),
            pltpu.VMEM((MIX_TM, 2 * CONV_CH), BF16),
            pltpu.VMEM((MIX_TM, SG_CH), BF16),
            pltpu.VMEM((SG_HEADS, CHUNK, CHUNK), BF16),
            pltpu.VMEM((CHUNK, SG_CH), F32),
        ],
        compiler_params=pltpu.CompilerParams(
            dimension_semantics=("arbitrary",),
            vmem_limit_bytes=VMEM_LIMIT_BYTES,
        ),
        name="mixer_ln",
    )(x, w_in, conv_w, conv_b, conv_ln_g, conv_ln_b, sg_ln_g, sg_ln_b,
      sg_w, sg_b_t, w_out, ln_g, ln_b)


def kernel(x, ffn1_w_gate_up, ffn1_w_down, ln1_g, ln1_b, mix_w_in, conv_w, conv_b, conv_ln_g, conv_ln_b, sg_ln_g, sg_ln_b, sg_w, sg_b, mix_w_out, ln2_g, ln2_b, ffn2_w_gate_up, ffn2_w_down, ln3_g, ln3_b):
    bsz, seq, d = x.shape
    h = x.reshape(bsz * seq, d)
    for l in range(ffn1_w_gate_up.shape[0]):
        row = lambda a: a[l].reshape(1, -1)
        h = _ffn_ln(h, ffn1_w_gate_up[l].astype(BF16), ffn1_w_down[l].astype(BF16),
                    row(ln1_g), row(ln1_b))
        h = _mixer_ln(h, mix_w_in[l].astype(BF16), conv_w[l], row(conv_b), row(conv_ln_g),
                      row(conv_ln_b), row(sg_ln_g), row(sg_ln_b), sg_w[l], sg_b[l].T,
                      mix_w_out[l].astype(BF16), row(ln2_g), row(ln2_b))
        h = _ffn_ln(h, ffn2_w_gate_up[l].astype(BF16), ffn2_w_down[l].astype(BF16),
                    row(ln3_g), row(ln3_b))
    return h.reshape(bsz, seq, d)
```

```python
import functools
import math

import jax
import jax.numpy as jnp
from jax import lax
from jax.experimental import pallas as pl
from jax.experimental.pallas import tpu as pltpu

D_MODEL = 2048
SEQ = 4096
CONV_CH = 1024
CONV_WIDTH = 31
SG_CH = 1024
SG_HEADS = 8
SG_HEAD_DIM = 128
CHUNK = 128
D_FF = 5632
LN_EPS = 1e-5
DN_ALPHA = 2.0 ** 0.25
GELU_C = math.sqrt(2.0 / math.pi)

LANES = 128
HALO = 32
VMEM_LIMIT_BYTES = 58 * 1024 * 1024

FFN_TM = 1024
FFN_TF = 512
MIX_TM = 256
LN_ROWS = 64
CONV_ROWS = 32
ACT_ROWS = 32

F32 = jnp.float32
BF16 = jnp.bfloat16


def _sigmoid(x):
    return 1.0 / (1.0 + jnp.exp(-x))


def _gelu_tanh(x):
    return x * (0.5 * (1.0 + jnp.tanh(GELU_C * (x + 0.044715 * (x * x * x)))))


def _layer_norm(y, g, b):
    mu = jnp.mean(y, axis=-1, keepdims=True)
    yc = y - mu
    var = jnp.mean(yc * yc, axis=-1, keepdims=True)
    return yc * lax.rsqrt(var + LN_EPS) * g + b


def _ffn_ln_kernel(x_ref, wg_ref, wu_ref, wd_ref, g_ref, b_ref, o_ref, xb_ref):
    j = pl.program_id(1)

    @pl.when(j == 0)
    def _():
        xb_ref[...] = x_ref[...].astype(BF16)
        o_ref[...] = jnp.zeros_like(o_ref)

    xb = xb_ref[...]
    gate = jnp.dot(xb, wg_ref[...], preferred_element_type=F32)
    up = jnp.dot(xb, wu_ref[...], preferred_element_type=F32)
    h = (gate * _sigmoid(gate) * up).astype(BF16)
    o_ref[...] += jnp.dot(h, wd_ref[...], preferred_element_type=F32)

    @pl.when(j == pl.num_programs(1) - 1)
    def _():
        def body(r, carry):
            rows = pl.ds(pl.multiple_of(r * LN_ROWS, LN_ROWS), LN_ROWS)
            y = DN_ALPHA * x_ref[rows, :] + 0.5 * o_ref[rows, :]
            o_ref[rows, :] = _layer_norm(y, g_ref[...], b_ref[...])
            return carry
        lax.fori_loop(0, FFN_TM // LN_ROWS, body, 0)


def _ffn_ln(x, w_gate_up, w_down, ln_g, ln_b):
    m = x.shape[0]
    nf = D_FF // FFN_TF
    return pl.pallas_call(
        _ffn_ln_kernel,
        out_shape=jax.ShapeDtypeStruct((m, D_MODEL), F32),
        grid=(m // FFN_TM, nf),
        in_specs=[
            pl.BlockSpec((FFN_TM, D_MODEL), lambda i, j: (i, 0)),
            pl.BlockSpec((D_MODEL, FFN_TF), lambda i, j: (0, j)),
            pl.BlockSpec((D_MODEL, FFN_TF), lambda i, j: (0, j + nf)),
            pl.BlockSpec((FFN_TF, D_MODEL), lambda i, j: (j, 0)),
            pl.BlockSpec((1, D_MODEL), lambda i, j: (0, 0)),
            pl.BlockSpec((1, D_MODEL), lambda i, j: (0, 0)),
        ],
        out_specs=pl.BlockSpec((FFN_TM, D_MODEL), lambda i, j: (i, 0)),
        scratch_shapes=[pltpu.VMEM((FFN_TM, D_MODEL), BF16)],
        compiler_params=pltpu.CompilerParams(
            dimension_semantics=("arbitrary", "arbitrary"),
            vmem_limit_bytes=VMEM_LIMIT_BYTES,
        ),
        name="ffn_ln",
    )(x, w_gate_up, w_gate_up, w_down, ln_g, ln_b)


PIPE_DEPTH = 1


def _mixer_ln_kernel(x_ref, xres_ref, win_ref, cw_ref, cb_ref, cg_ref, cbeta_ref,
                     sg_g_ref, sg_beta_ref, sgw_ref, sgbt_ref, wout_ref,
                     g_ref, b_ref, o_ref,
                     proj_ref, hist_ref, u_ref, vb_ref, y_ref, wm_ref, bias_ref):
    q = pl.program_id(0)
    tiles_per_seq = SEQ // MIX_TM
    n_ct = CONV_CH // LANES

    @pl.when(q == 0)
    def _():
        t_idx = lax.broadcasted_iota(jnp.int32, (CHUNK, CHUNK), 0)
        s_idx = lax.broadcasted_iota(jnp.int32, (CHUNK, CHUNK), 1)
        for h in range(SG_HEADS):
            wm_ref[h] = jnp.where(s_idx <= t_idx, sgw_ref[h], 0.0).astype(BF16)
            bias_ref[:, h * LANES:(h + 1) * LANES] = jnp.broadcast_to(
                sgbt_ref[:, h:h + 1], (CHUNK, LANES))
        proj_ref[...] = jnp.zeros_like(proj_ref)
        hist_ref[...] = jnp.zeros_like(hist_ref)

    @pl.when((q + tiles_per_seq - 1) % tiles_per_seq == 0)
    def _():
        hist_ref[:, 0:HALO, :] = jnp.zeros((n_ct, HALO, LANES), F32)

    serial = jnp.minimum(q, 0)

    for r0 in range(0, MIX_TM, ACT_ROWS):
        rows = slice(r0, r0 + ACT_ROWS)
        hval = proj_ref[rows, 0:CONV_CH] * _sigmoid(proj_ref[rows, CONV_CH:2 * CONV_CH])
        for c in range(n_ct):
            hist_ref[c + serial, HALO + r0:HALO + r0 + ACT_ROWS, :] = hval[:, c * LANES:(c + 1) * LANES]
    for r0 in range(0, MIX_TM, ACT_ROWS):
        rows = slice(r0, r0 + ACT_ROWS)
        u_ref[rows, :] = _gelu_tanh(proj_ref[rows, 2 * CONV_CH:2 * CONV_CH + SG_CH])
        v = _gelu_tanh(proj_ref[rows, 2 * CONV_CH + SG_CH:2 * CONV_CH + 2 * SG_CH])
        for h in range(SG_HEADS):
            lanes = slice(h * LANES, (h + 1) * LANES)
            vb_ref[rows, lanes] = _layer_norm(
                v[:, lanes], sg_g_ref[:, lanes], sg_beta_ref[:, lanes]).astype(BF16)

    proj_ref[...] = jnp.dot(x_ref[...].astype(BF16), win_ref[...], preferred_element_type=F32)

    tap0 = HALO - (CONV_WIDTH - 1)
    for c in range(n_ct):
        lanes = slice(c * LANES, (c + 1) * LANES)
        taps = [cw_ref[k:k + 1, lanes] for k in range(CONV_WIDTH)]
        cbias = cb_ref[:, lanes]
        for r0 in range(0, MIX_TM, CONV_ROWS):
            acc = taps[0] * hist_ref[c + serial, r0 + tap0:r0 + tap0 + CONV_ROWS, :]
            for k in range(1, CONV_WIDTH):
                acc = acc + taps[k] * hist_ref[c + serial, r0 + tap0 + k:r0 + tap0 + k + CONV_ROWS, :]
            hist_ref[c + serial, r0:r0 + CONV_ROWS, :] = acc + cbias

    for r0 in range(0, MIX_TM, LN_ROWS):
        rows = slice(r0, r0 + LN_ROWS)
        conv = jnp.concatenate([hist_ref[c + serial, rows, :] for c in range(n_ct)], axis=-1)
        z = _layer_norm(conv, cg_ref[...], cbeta_ref[...])
        y_ref[rows, 0:CONV_CH] = (z * _sigmoid(z)).astype(BF16)
    for c in range(n_ct):
        hist_ref[c + serial, 0:HALO, :] = hist_ref[c + serial, MIX_TM:MIX_TM + HALO, :]

    for ck in range(MIX_TM // CHUNK):
        rows = slice(ck * CHUNK, (ck + 1) * CHUNK)
        for h in range(SG_HEADS):
            lanes = slice(h * LANES, (h + 1) * LANES)
            mixed = jnp.dot(wm_ref[h], vb_ref[rows, lanes], preferred_element_type=F32)
            y_ref[rows, CONV_CH + h * LANES:CONV_CH + (h + 1) * LANES] = (
                u_ref[rows, lanes] * (mixed + bias_ref[:, lanes])).astype(BF16)

    mix = jnp.dot(y_ref[...], wout_ref[...], preferred_element_type=F32)
    for r0 in range(0, MIX_TM, LN_ROWS):
        rows = slice(r0, r0 + LN_ROWS)
        o_ref[rows, :] = _layer_norm(DN_ALPHA * xres_ref[rows, :] + mix[rows, :],
                                     g_ref[...], b_ref[...])


def _mixer_ln(x, w_in, conv_w, conv_b, conv_ln_g, conv_ln_b, sg_ln_g, sg_ln_b,
              sg_w, sg_b_t, w_out, ln_g, ln_b):
    m = x.shape[0]
    n_tiles = m // MIX_TM
    const2 = lambda q: (0, 0)
    head = lambda q: (jnp.minimum(q, n_tiles - 1), 0)
    tail = lambda q: (jnp.maximum(q - PIPE_DEPTH, 0), 0)
    resident = functools.partial(pl.BlockSpec, pipeline_mode=pl.Buffered(1))
    return pl.pallas_call(
        _mixer_ln_kernel,
        out_shape=jax.ShapeDtypeStruct((m, D_MODEL), F32),
        grid=(n_tiles + PIPE_DEPTH,),
        in_specs=[
            pl.BlockSpec((MIX_TM, D_MODEL), head),
            pl.BlockSpec((MIX_TM, D_MODEL), tail),
            resident((D_MODEL, 2 * CONV_CH + 2 * SG_CH), const2),
            pl.BlockSpec((CONV_WIDTH, CONV_CH), const2),
            pl.BlockSpec((1, CONV_CH), const2),
            pl.BlockSpec((1, CONV_CH), const2),
            pl.BlockSpec((1, CONV_CH), const2),
            pl.BlockSpec((1, SG_CH), const2),
            pl.BlockSpec((1, SG_CH), const2),
            pl.BlockSpec((SG_HEADS, CHUNK, CHUNK), lambda q: (0, 0, 0)),
            pl.BlockSpec((CHUNK, SG_HEADS), const2),
            resident((2 * CONV_CH, D_MODEL), const2),
            pl.BlockSpec((1, D_MODEL), const2),
            pl.BlockSpec((1, D_MODEL), const2),
        ],
        out_specs=pl.BlockSpec((MIX_TM, D_MODEL), tail),
        scratch_shapes=[
            pltpu.VMEM((MIX_TM, 2 * CONV_CH + 2 * SG_CH), F32),
            pltpu.VMEM((CONV_CH // LANES, HALO + MIX_TM, LANES), F32),
            pltpu.VMEM((MIX_TM, SG_CH), F32),
            pltpu.VMEM((MIX_TM, SG_CH), BF16),
            pltpu.VMEM((MIX_TM, 2 * CONV_CH), BF16),
            pltpu.VMEM((SG_HEADS, CHUNK, CHUNK), BF16),
            pltpu.VMEM((CHUNK, SG_CH), F32),
        ],
        compiler_params=pltpu.CompilerParams(
            dimension_semantics=("arbitrary",),
            vmem_limit_bytes=VMEM_LIMIT_BYTES,
        ),
        name="mixer_ln",
    )(x, x, w_in, conv_w, conv_b, conv_ln_g, conv_ln_b, sg_ln_g, sg_ln_b,
      sg_w, sg_b_t, w_out, ln_g, ln_b)


def kernel(x, ffn1_w_gate_up, ffn1_w_down, ln1_g, ln1_b, mix_w_in, conv_w, conv_b, conv_ln_g, conv_ln_b, sg_ln_g, sg_ln_b, sg_w, sg_b, mix_w_out, ln2_g, ln2_b, ffn2_w_gate_up, ffn2_w_down, ln3_g, ln3_b):
    bsz, seq, d = x.shape
    h = x.reshape(bsz * seq, d)
    for l in range(ffn1_w_gate_up.shape[0]):
        row = lambda a: a[l].reshape(1, -1)
        h = _ffn_ln(h, ffn1_w_gate_up[l].astype(BF16), ffn1_w_down[l].astype(BF16),
                    row(ln1_g), row(ln1_b))
        h = _mixer_ln(h, mix_w_in[l].astype(BF16), conv_w[l], row(conv_b), row(conv_ln_g),
                      row(conv_ln_b), row(sg_ln_g), row(sg_ln_b), sg_w[l], sg_b[l].T,
                      mix_w_out[l].astype(BF16), row(ln2_g), row(ln2_b))
        h = _ffn_ln(h, ffn2_w_gate_up[l].astype(BF16), ffn2_w_down[l].astype(BF16),
                    row(ln3_g), row(ln3_b))
    return h.reshape(bsz, seq, d)
```

```python
import functools
import math

import jax
import jax.numpy as jnp
from jax import lax
from jax.experimental import pallas as pl
from jax.experimental.pallas import tpu as pltpu

D_MODEL = 2048
SEQ = 4096
CONV_CH = 1024
CONV_WIDTH = 31
SG_CH = 1024
SG_HEADS = 8
SG_HEAD_DIM = 128
CHUNK = 128
D_FF = 5632
LN_EPS = 1e-5
DN_ALPHA = 2.0 ** 0.25
GELU_C = math.sqrt(2.0 / math.pi)

LANES = 128
BF16_SUBLANES = 16
HALO = 32
VMEM_BYTES = 64 * 1024 * 1024
VMEM_TEMP_BYTES = 8 * 1024 * 1024
PIPELINE_BUFFERS = 2

FFN_TM = 1024
FFN_TF = 512
MIX_TM = 256
LN_ROWS = 64
FFN_LN_ROWS = 128
CONV_ROWS = 32
ACT_ROWS = 32
DOT_COLS = 512

F32 = jnp.float32
BF16 = jnp.bfloat16


def _vmem_limit(buffer_bytes):
    limit = buffer_bytes + VMEM_TEMP_BYTES
    assert limit <= VMEM_BYTES, limit
    return limit


def _sigmoid(x):
    return 1.0 / (1.0 + jnp.exp(-x))


def _gelu_tanh(x):
    return x * (0.5 * (1.0 + jnp.tanh(GELU_C * (x + 0.044715 * (x * x * x)))))


def _layer_norm(y, g, b):
    mu = jnp.mean(y, axis=-1, keepdims=True)
    yc = y - mu
    var = jnp.mean(yc * yc, axis=-1, keepdims=True)
    return yc * lax.rsqrt(var + LN_EPS) * g + b


def _ffn_ln_kernel(*refs, n_cast):
    x_ref, wg_ref, wu_ref, wd_ref, g_ref, b_ref = refs[:6]
    cast_src = refs[6:6 + n_cast]
    o_ref = refs[6 + n_cast]
    cast_dst = refs[7 + n_cast:]
    j = pl.program_id(1)

    @pl.when(j == 0)
    def _():
        o_ref[...] = jnp.zeros_like(o_ref)

    xb = x_ref[...].astype(BF16)
    gate = jnp.dot(xb, wg_ref[...], preferred_element_type=F32)
    up = jnp.dot(xb, wu_ref[...], preferred_element_type=F32)
    h = (gate * _sigmoid(gate) * up).astype(BF16)
    o_ref[...] += jnp.dot(h, wd_ref[...], preferred_element_type=F32)

    for src, dst in zip(cast_src, cast_dst):
        if len(dst.shape) == 2:
            dst[...] = src[...].astype(BF16)
        else:
            cols = dst.shape[2]
            for n in range(dst.shape[0]):
                dst[n] = src[:, n * cols:(n + 1) * cols].astype(BF16)

    @pl.when(j == pl.num_programs(1) - 1)
    def _():
        def body(r, carry):
            rows = pl.ds(pl.multiple_of(r * FFN_LN_ROWS, FFN_LN_ROWS), FFN_LN_ROWS)
            y = DN_ALPHA * x_ref[rows, :] + 0.5 * o_ref[rows, :]
            o_ref[rows, :] = _layer_norm(y, g_ref[...], b_ref[...])
            return carry
        lax.fori_loop(0, FFN_TM // FFN_LN_ROWS, body, 0)


def _cast_job_specs(shape, col_block, n_steps, nf):
    rows = next(r for r in range(BF16_SUBLANES, shape[0] + 1, BF16_SUBLANES)
                if shape[0] % r == 0 and shape[0] // r <= n_steps)
    n_blocks = shape[0] // rows
    step = lambda i, j: jnp.minimum(i * nf + j, n_blocks - 1)
    in_spec = pl.BlockSpec((rows, shape[1]), lambda i, j: (step(i, j), 0))
    if col_block is None:
        return in_spec, in_spec, shape
    n_cb = shape[1] // col_block
    out_spec = pl.BlockSpec((n_cb, rows, col_block), lambda i, j: (0, step(i, j), 0))
    return in_spec, out_spec, (n_cb, shape[0], col_block)


def _ffn_ln(x, w_gate_up, w_down, ln_g, ln_b, cast_jobs=()):
    m = x.shape[0]
    nf = D_FF // FFN_TF
    cast_specs = [_cast_job_specs(a.shape, cb, (m // FFN_TM) * nf, nf) for a, cb in cast_jobs]
    window_bytes = (2 * FFN_TM * D_MODEL * 4
                    + 3 * D_MODEL * FFN_TF * 2
                    + sum(math.prod(s[0].block_shape) * (4 + 2) for s in cast_specs))
    outs = pl.pallas_call(
        functools.partial(_ffn_ln_kernel, n_cast=len(cast_jobs)),
        out_shape=[jax.ShapeDtypeStruct((m, D_MODEL), F32)]
        + [jax.ShapeDtypeStruct(s[2], BF16) for s in cast_specs],
        grid=(m // FFN_TM, nf),
        in_specs=[
            pl.BlockSpec((FFN_TM, D_MODEL), lambda i, j: (i, 0)),
            pl.BlockSpec((D_MODEL, FFN_TF), lambda i, j: (0, j)),
            pl.BlockSpec((D_MODEL, FFN_TF), lambda i, j: (0, j + nf)),
            pl.BlockSpec((FFN_TF, D_MODEL), lambda i, j: (j, 0)),
            pl.BlockSpec((1, D_MODEL), lambda i, j: (0, 0)),
            pl.BlockSpec((1, D_MODEL), lambda i, j: (0, 0)),
        ] + [s[0] for s in cast_specs],
        out_specs=[pl.BlockSpec((FFN_TM, D_MODEL), lambda i, j: (i, 0))] + [s[1] for s in cast_specs],
        compiler_params=pltpu.CompilerParams(
            dimension_semantics=("arbitrary", "arbitrary"),
            vmem_limit_bytes=_vmem_limit(PIPELINE_BUFFERS * window_bytes),
        ),
        name="ffn_ln",
    )(x, w_gate_up, w_gate_up, w_down, ln_g, ln_b, *[a for a, _ in cast_jobs])
    return outs[0], tuple(outs[1:])


PIPE_DEPTH = 1


def _mixer_ln_kernel(x_ref, xres_ref, win_ref, cw_ref, cb_ref, cg_ref, cbeta_ref,
                     sg_g_ref, sg_beta_ref, sgw_ref, sgbt_ref, wout_ref,
                     g_ref, b_ref, o_ref,
                     proj_ref, hist_ref, u_ref, vb_ref, y_ref, wm_ref, bias_ref):
    q = pl.program_id(0)
    tiles_per_seq = SEQ // MIX_TM
    n_ct = CONV_CH // LANES

    @pl.when(q == 0)
    def _():
        t_idx = lax.broadcasted_iota(jnp.int32, (CHUNK, CHUNK), 0)
        s_idx = lax.broadcasted_iota(jnp.int32, (CHUNK, CHUNK), 1)
        for h in range(SG_HEADS):
            wm_ref[h] = jnp.where(s_idx <= t_idx, sgw_ref[h], 0.0).astype(BF16)
            bias_ref[:, h * LANES:(h + 1) * LANES] = jnp.broadcast_to(
                sgbt_ref[:, h:h + 1], (CHUNK, LANES))
        proj_ref[...] = jnp.zeros_like(proj_ref)
        hist_ref[...] = jnp.zeros_like(hist_ref)

    @pl.when((q + tiles_per_seq - 1) % tiles_per_seq == 0)
    def _():
        hist_ref[0:n_ct, 0:HALO, :] = jnp.zeros((n_ct, HALO, LANES), F32)

    serial = jnp.minimum(q, 0)

    for r0 in range(0, MIX_TM, ACT_ROWS):
        rows = slice(r0, r0 + ACT_ROWS)
        hval = proj_ref[rows, 0:CONV_CH] * _sigmoid(proj_ref[rows, CONV_CH:2 * CONV_CH])
        for c in range(n_ct):
            hist_ref[c + serial, HALO + r0:HALO + r0 + ACT_ROWS, :] = hval[:, c * LANES:(c + 1) * LANES]
    for r0 in range(0, MIX_TM, ACT_ROWS):
        rows = slice(r0, r0 + ACT_ROWS)
        u_ref[rows, :] = _gelu_tanh(proj_ref[rows, 2 * CONV_CH:2 * CONV_CH + SG_CH])
        v = _gelu_tanh(proj_ref[rows, 2 * CONV_CH + SG_CH:2 * CONV_CH + 2 * SG_CH])
        for h in range(SG_HEADS):
            lanes = slice(h * LANES, (h + 1) * LANES)
            vb_ref[rows, lanes] = _layer_norm(
                v[:, lanes], sg_g_ref[:, lanes], sg_beta_ref[:, lanes]).astype(BF16)

    xb = x_ref[...].astype(BF16)
    for n0 in range(0, 2 * CONV_CH + 2 * SG_CH, DOT_COLS):
        proj_ref[:, n0:n0 + DOT_COLS] = jnp.dot(xb, win_ref[n0 // DOT_COLS],
                                                preferred_element_type=F32)

    tap0 = HALO - (CONV_WIDTH - 1)
    for c in range(n_ct):
        lanes = slice(c * LANES, (c + 1) * LANES)
        taps = [cw_ref[k:k + 1, lanes] for k in range(CONV_WIDTH)]
        cbias = cb_ref[:, lanes]
        for r0 in range(0, MIX_TM, CONV_ROWS):
            acc = taps[0] * hist_ref[c + serial, r0 + tap0:r0 + tap0 + CONV_ROWS, :]
            for k in range(1, CONV_WIDTH):
                acc = acc + taps[k] * hist_ref[c + serial, r0 + tap0 + k:r0 + tap0 + k + CONV_ROWS, :]
            hist_ref[c + serial, r0:r0 + CONV_ROWS, :] = acc + cbias

    for r0 in range(0, MIX_TM, LN_ROWS):
        rows = slice(r0, r0 + LN_ROWS)
        conv = jnp.concatenate([hist_ref[c + serial, rows, :] for c in range(n_ct)], axis=-1)
        z = _layer_norm(conv, cg_ref[...], cbeta_ref[...])
        y_ref[rows, 0:CONV_CH] = (z * _sigmoid(z)).astype(BF16)
    for c in range(n_ct):
        hist_ref[c + serial, 0:HALO, :] = hist_ref[c + serial, MIX_TM:MIX_TM + HALO, :]

    for ck in range(MIX_TM // CHUNK):
        rows = slice(ck * CHUNK, (ck + 1) * CHUNK)
        for h in range(SG_HEADS):
            lanes = slice(h * LANES, (h + 1) * LANES)
            mixed = jnp.dot(wm_ref[h], vb_ref[rows, lanes], preferred_element_type=F32)
            y_ref[rows, CONV_CH + h * LANES:CONV_CH + (h + 1) * LANES] = (
                u_ref[rows, lanes] * (mixed + bias_ref[:, lanes])).astype(BF16)

    y = y_ref[...]
    mix = jnp.concatenate(
        [jnp.dot(y, wout_ref[n0 // DOT_COLS], preferred_element_type=F32)
         for n0 in range(0, D_MODEL, DOT_COLS)], axis=-1)
    for r0 in range(0, MIX_TM, LN_ROWS):
        rows = slice(r0, r0 + LN_ROWS)
        o_ref[rows, :] = _layer_norm(DN_ALPHA * xres_ref[rows, :] + mix[rows, :],
                                     g_ref[...], b_ref[...])


def _mixer_ln(x, w_in, conv_w, conv_b, conv_ln_g, conv_ln_b, sg_ln_g, sg_ln_b,
              sg_w, sg_b_t, w_out, ln_g, ln_b):
    m = x.shape[0]
    n_tiles = m // MIX_TM
    const2 = lambda q: (0, 0)
    const3 = lambda q: (0, 0, 0)
    head = lambda q: (jnp.minimum(q, n_tiles - 1), 0)
    tail = lambda q: (jnp.maximum(q - PIPE_DEPTH, 0), 0)
    resident = functools.partial(pl.BlockSpec, pipeline_mode=pl.Buffered(1))
    scratch = [
        pltpu.VMEM((MIX_TM, 2 * CONV_CH + 2 * SG_CH), F32),
        pltpu.VMEM((CONV_CH // LANES, HALO + MIX_TM, LANES), F32),
        pltpu.VMEM((MIX_TM, SG_CH), F32),
        pltpu.VMEM((MIX_TM, SG_CH), BF16),
        pltpu.VMEM((MIX_TM, 2 * CONV_CH), BF16),
        pltpu.VMEM((SG_HEADS, CHUNK, CHUNK), BF16),
        pltpu.VMEM((CHUNK, SG_CH), F32),
    ]
    buffer_bytes = (3 * PIPELINE_BUFFERS * MIX_TM * D_MODEL * 4
                    + (w_in.size + w_out.size) * 2
                    + PIPELINE_BUFFERS * (sg_w.size + conv_w.size) * 4
                    + sum(math.prod(s.shape) * s.dtype.itemsize for s in scratch))
    return pl.pallas_call(
        _mixer_ln_kernel,
        out_shape=jax.ShapeDtypeStruct((m, D_MODEL), F32),
        grid=(n_tiles + PIPE_DEPTH,),
        in_specs=[
            pl.BlockSpec((MIX_TM, D_MODEL), head),
            pl.BlockSpec((MIX_TM, D_MODEL), tail),
            resident(w_in.shape, const3),
            pl.BlockSpec((CONV_WIDTH, CONV_CH), const2),
            pl.BlockSpec((1, CONV_CH), const2),
            pl.BlockSpec((1, CONV_CH), const2),
            pl.BlockSpec((1, CONV_CH), const2),
            pl.BlockSpec((1, SG_CH), const2),
            pl.BlockSpec((1, SG_CH), const2),
            pl.BlockSpec((SG_HEADS, CHUNK, CHUNK), const3),
            pl.BlockSpec((CHUNK, SG_HEADS), const2),
            resident(w_out.shape, const3),
            pl.BlockSpec((1, D_MODEL), const2),
            pl.BlockSpec((1, D_MODEL), const2),
        ],
        out_specs=pl.BlockSpec((MIX_TM, D_MODEL), tail),
        scratch_shapes=scratch,
        compiler_params=pltpu.CompilerParams(
            dimension_semantics=("arbitrary",),
            vmem_limit_bytes=_vmem_limit(buffer_bytes),
        ),
        name="mixer_ln",
    )(x, x, w_in, conv_w, conv_b, conv_ln_g, conv_ln_b, sg_ln_g, sg_ln_b,
      sg_w, sg_b_t, w_out, ln_g, ln_b)


def kernel(x, ffn1_w_gate_up, ffn1_w_down, ln1_g, ln1_b, mix_w_in, conv_w, conv_b, conv_ln_g, conv_ln_b, sg_ln_g, sg_ln_b, sg_w, sg_b, mix_w_out, ln2_g, ln2_b, ffn2_w_gate_up, ffn2_w_down, ln3_g, ln3_b):
    bsz, seq, d = x.shape
    h = x.reshape(bsz * seq, d)
    for l in range(ffn1_w_gate_up.shape[0]):
        row = lambda a: a[l].reshape(1, -1)
        cast_jobs = ((ffn2_w_gate_up[l], None), (ffn2_w_down[l], None),
                     (mix_w_in[l], DOT_COLS), (mix_w_out[l], DOT_COLS))
        h, (w2_gate_up, w2_down, w_in, w_out) = _ffn_ln(
            h, ffn1_w_gate_up[l].astype(BF16), ffn1_w_down[l].astype(BF16),
            row(ln1_g), row(ln1_b), cast_jobs)
        h = _mixer_ln(h, w_in, conv_w[l], row(conv_b), row(conv_ln_g), row(conv_ln_b),
                      row(sg_ln_g), row(sg_ln_b), sg_w[l], sg_b[l].T, w_out,
                      row(ln2_g), row(ln2_b))
        h, _ = _ffn_ln(h, w2_gate_up, w2_down, row(ln3_g), row(ln3_b))
    return h.reshape(bsz, seq, d)
```
